```python
import math
import jax
import jax.numpy as jnp
from jax import lax
import numpy as np

D_MODEL = 1024
BATCH = 32
SEQ = 2048
DEPTH = 1
DEC_BATCH = 32
DEC_SEQ = 32
PAST_LEN = 4096

CHUNK = 64
D_MIX = D_MODEL
D_MLSTM = D_MIX // 2
N_MLSTM_HEADS = 4
DH = D_MLSTM // N_MLSTM_HEADS
D_POOL = D_MIX - D_MLSTM
POOL_WINDOWS = (2, 4, 8, 16)
N_POOL_GROUPS = len(POOL_WINDOWS)
D_POOL_GROUP = D_POOL // N_POOL_GROUPS
POOL_STATE = max(POOL_WINDOWS) - 1
D_IN = 4 * D_MLSTM + 2 * N_MLSTM_HEADS + D_POOL
N_EXPERT_GROUPS = 4
EXPERTS_PER_GROUP = 4
N_EXPERTS = N_EXPERT_GROUPS * EXPERTS_PER_GROUP
TOP_K_INNER = 2
D_EXPERT = 512
EPS = 1e-6
M_INIT = -1e30

kernel_name = "hymba_mlstm_multipool_hiermoe_stream_step"


def rmsnorm(x, g):
    xf = x.astype(jnp.float32)
    r = lax.rsqrt(jnp.mean(xf * xf, axis=-1, keepdims=True) + EPS)
    return (xf * r).astype(x.dtype) * g


def mlstm_chunk_step(carry, inp):
    C, n, m = carry
    q, k, v, gi, lf = inp
    L = q.shape[2]
    b = jnp.cumsum(lf, axis=-1)
    causal = jnp.tril(jnp.ones((L, L), dtype=bool))
    log_d = jnp.where(causal, b[..., :, None] - b[..., None, :] + gi[..., None, :], -jnp.inf)
    log_inter = b + m[..., None]
    m_t = jnp.maximum(log_inter, jnp.max(log_d, axis=-1))
    d = jnp.exp(log_d - m_t[..., None])
    inter = jnp.exp(log_inter - m_t)
    s = jnp.einsum('bhtd,bhsd->bhts', q, k) * d
    num = inter[..., None] * jnp.einsum('bhtd,bhdv->bhtv', q, C) + jnp.einsum('bhts,bhsv->bhtv', s, v)
    den = inter * jnp.einsum('bhtd,bhd->bht', q, n) + jnp.sum(s, axis=-1)
    h = num / jnp.maximum(jnp.abs(den), jnp.exp(-m_t))[..., None]
    m_new = m_t[..., -1]
    w_s = jnp.exp(b[..., -1:] - b + gi - m_new[..., None])
    decay = jnp.exp(b[..., -1] + m - m_new)
    C_new = decay[..., None, None] * C + jnp.einsum('bhs,bhsd,bhsv->bhdv', w_s, k, v)
    n_new = decay[..., None] * n + jnp.einsum('bhs,bhsd->bhd', w_s, k)
    return (C_new, n_new, m_new), h


def mlstm_run(q, k, v, gi, lf, C, n, m):
    L = q.shape[2]
    if L <= CHUNK:
        (C, n, m), h = mlstm_chunk_step((C, n, m), (q, k, v, gi, lf))
        return h, C, n, m

    def to_chunks(a):
        a = a.reshape(a.shape[0], a.shape[1], L // CHUNK, CHUNK, *a.shape[3:])
        return jnp.moveaxis(a, 2, 0)

    (C, n, m), hs = lax.scan(mlstm_chunk_step, (C, n, m),
                             (to_chunks(q), to_chunks(k), to_chunks(v), to_chunks(gi), to_chunks(lf)))
    h = jnp.moveaxis(hs, 0, 2).reshape(q.shape)
    return h, C, n, m


def pool_mix(u, buf, pos0, pool_w, pool_scale):
    B, L, _ = u.shape
    full = jnp.concatenate([buf.astype(u.dtype), u], axis=1)
    uf = full.astype(jnp.float32)
    cs = jnp.concatenate([jnp.zeros((B, 1, D_POOL), jnp.float32), jnp.cumsum(uf, axis=1)], axis=1)
    pos = pos0 + jnp.arange(L)
    outs = []
    for g, w in enumerate(POOL_WINDOWS):
        sl = slice(g * D_POOL_GROUP, (g + 1) * D_POOL_GROUP)
        s = cs[:, POOL_STATE + 1:POOL_STATE + 1 + L, sl] - cs[:, POOL_STATE + 1 - w:POOL_STATE + 1 - w + L, sl]
        cnt = jnp.minimum(pos + 1, w).astype(jnp.float32)
        dlt = s / cnt[None, :, None] - uf[:, POOL_STATE:, sl]
        outs.append(jnp.einsum('bld,de->ble', dlt, pool_w[g].astype(jnp.float32)))
    out = jnp.concatenate(outs, axis=-1) * pool_scale.astype(jnp.float32)
    return out.astype(u.dtype), full[:, -POOL_STATE:, :]


def hier_moe(x, w_rg, b_rg, w_re, b_re, w_gate, w_up, w_down):
    B, L, _ = x.shape
    pg = jax.nn.softmax(jnp.einsum('bld,dg->blg', x, w_rg).astype(jnp.float32) + b_rg, axis=-1)
    pg_top, g_idx = lax.top_k(pg, 1)
    le = (jnp.einsum('bld,de->ble', x, w_re).astype(jnp.float32) + b_re).reshape(B, L, N_EXPERT_GROUPS, EXPERTS_PER_GROUP)
    le = jnp.take_along_axis(le, g_idx[..., None], axis=-2)[..., 0, :]
    pe = jax.nn.softmax(le, axis=-1)
    pv, pi = lax.top_k(pe, TOP_K_INNER)
    pv = pv / jnp.sum(pv, axis=-1, keepdims=True)
    inner = jnp.einsum('blk,blke->ble', pv, jax.nn.one_hot(pi, EXPERTS_PER_GROUP, dtype=jnp.float32))
    gates = (pg_top[..., None] * jax.nn.one_hot(g_idx[..., 0], N_EXPERT_GROUPS, dtype=jnp.float32)[..., None]
             * inner[..., None, :]).reshape(B, L, N_EXPERTS).astype(x.dtype)
    y = jnp.zeros_like(x)
    for e in range(N_EXPERTS):
        hid = jax.nn.silu(x @ w_gate[e]) * (x @ w_up[e])
        y = y + gates[..., e:e + 1] * (hid @ w_down[e])
    return y


def trunk_layer(x, C, n, m, buf, pos0, norm1_g, w_in, b_gates, mh_norm_g, pool_w, pool_scale, w_out,
                norm2_g, w_rg, b_rg, w_re, b_re, w_gate, w_up, w_down):
    B, L, _ = x.shape
    H = N_MLSTM_HEADS
    h = rmsnorm(x, norm1_g)
    p = h @ w_in

    def heads(a):
        return a.reshape(B, L, H, DH).transpose(0, 2, 1, 3).astype(jnp.float32)

    q = heads(p[..., 0:D_MLSTM])
    k = heads(p[..., D_MLSTM:2 * D_MLSTM]) * (DH ** -0.5)
    v = heads(p[..., 2 * D_MLSTM:3 * D_MLSTM])
    o = p[..., 3 * D_MLSTM:4 * D_MLSTM].astype(jnp.float32)
    bg = b_gates.astype(jnp.float32)
    gi = (p[..., 4 * D_MLSTM:4 * D_MLSTM + H].astype(jnp.float32) + bg[:H]).transpose(0, 2, 1)
    lf = jax.nn.log_sigmoid(p[..., 4 * D_MLSTM + H:4 * D_MLSTM + 2 * H].astype(jnp.float32) + bg[H:]).transpose(0, 2, 1)
    u = p[..., 4 * D_MLSTM + 2 * H:]

    hm, C, n, m = mlstm_run(q, k, v, gi, lf, C, n, m)
    hm = hm * lax.rsqrt(jnp.mean(hm * hm, axis=-1, keepdims=True) + EPS)
    hm = hm.transpose(0, 2, 1, 3).reshape(B, L, D_MLSTM) * mh_norm_g.astype(jnp.float32) * jax.nn.sigmoid(o)
    hp, buf = pool_mix(u, buf, pos0, pool_w, pool_scale)

    mix = jnp.concatenate([hm.astype(x.dtype), hp], axis=-1)
    x = x + mix @ w_out
    x = x + hier_moe(rmsnorm(x, norm2_g), w_rg, b_rg, w_re, b_re, w_gate, w_up, w_down)
    return x, C, n, m, buf


def setup_inputs(seed: int = 0) -> dict:
    key = jax.random.key(seed)
    ks = jax.random.split(key, 24)
    f32 = jnp.float32
    H = N_MLSTM_HEADS

    def nrm(k, shape, scale):
        return jax.random.normal(k, shape, f32) * scale

    b_gates = jnp.concatenate([nrm(ks[8], (DEPTH, H), 0.1),
                               jnp.linspace(3.0, 6.0, H, dtype=f32)[None, :] + nrm(ks[9], (DEPTH, H), 0.1)], axis=-1)
    return {
        'x_prompt': nrm(ks[0], (BATCH, SEQ, D_MODEL), 1.0),
        'x_sample': nrm(ks[1], (DEC_BATCH, DEC_SEQ, D_MODEL), 1.0),
        'state_mlstm_C': nrm(ks[2], (DEPTH, DEC_BATCH, H, DH, DH), 0.05),
        'state_mlstm_n': nrm(ks[3], (DEPTH, DEC_BATCH, H, DH), 0.05),
        'state_mlstm_m': nrm(ks[4], (DEPTH, DEC_BATCH, H), 1.0),
        'cache_pool': nrm(ks[5], (DEPTH, DEC_BATCH, POOL_STATE, D_POOL), 1.0),
        'norm1_g': 1.0 + nrm(ks[6], (DEPTH, D_MODEL), 0.02),
        'w_in': nrm(ks[7], (DEPTH, D_MODEL, D_IN), D_MODEL ** -0.5),
        'b_gates': b_gates,
        'mh_norm_g': 1.0 + nrm(ks[10], (DEPTH, D_MLSTM), 0.02),
        'pool_w': nrm(ks[11], (DEPTH, N_POOL_GROUPS, D_POOL_GROUP, D_POOL_GROUP), D_POOL_GROUP ** -0.5),
        'pool_scale': 1.0 + nrm(ks[12], (DEPTH, D_POOL), 0.1),
        'w_out': nrm(ks[13], (DEPTH, D_MIX, D_MODEL), D_MIX ** -0.5),
        'norm2_g': 1.0 + nrm(ks[14], (DEPTH, D_MODEL), 0.02),
        'w_router_group': nrm(ks[15], (DEPTH, D_MODEL, N_EXPERT_GROUPS), D_MODEL ** -0.5),
        'b_router_group': nrm(ks[16], (DEPTH, N_EXPERT_GROUPS), 0.01),
        'w_router_expert': nrm(ks[17], (DEPTH, D_MODEL, N_EXPERTS), D_MODEL ** -0.5),
        'b_router_expert': nrm(ks[18], (DEPTH, N_EXPERTS), 0.01),
        'w_gate': nrm(ks[19], (DEPTH, N_EXPERTS, D_MODEL, D_EXPERT), D_MODEL ** -0.5),
        'w_up': nrm(ks[20], (DEPTH, N_EXPERTS, D_MODEL, D_EXPERT), D_MODEL ** -0.5),
        'w_down': nrm(ks[21], (DEPTH, N_EXPERTS, D_EXPERT, D_MODEL), D_EXPERT ** -0.5),
        'norm_f_g': 1.0 + nrm(ks[22], (D_MODEL,), 0.02),
    }


def reference(x_prompt, x_sample, state_mlstm_C, state_mlstm_n, state_mlstm_m, cache_pool,
              norm1_g, w_in, b_gates, mh_norm_g, pool_w, pool_scale, w_out, norm2_g,
              w_router_group, b_router_group, w_router_expert, b_router_expert,
              w_gate, w_up, w_down, norm_f_g):
    B = x_prompt.shape[0]
    H = N_MLSTM_HEADS
    xp, xs = x_prompt, x_sample
    Cp_l, np_l, mp_l, bp_l = [], [], [], []
    Cs_l, ns_l, ms_l, bs_l = [], [], [], []
    for l in range(DEPTH):
        lw = (norm1_g[l], w_in[l], b_gates[l], mh_norm_g[l], pool_w[l], pool_scale[l], w_out[l], norm2_g[l],
              w_router_group[l], b_router_group[l], w_router_expert[l], b_router_expert[l],
              w_gate[l], w_up[l], w_down[l])
        C0 = jnp.zeros((B, H, DH, DH), jnp.float32)
        n0 = jnp.zeros((B, H, DH), jnp.float32)
        m0 = jnp.full((B, H), M_INIT, jnp.float32)
        buf0 = jnp.zeros((B, POOL_STATE, D_POOL), xp.dtype)
        xp, Cp, np_, mp, bp = trunk_layer(xp, C0, n0, m0, buf0, 0, *lw)
        xs, Cs, ns, ms, bs = trunk_layer(xs, state_mlstm_C[l].astype(jnp.float32), state_mlstm_n[l].astype(jnp.float32),
                                         state_mlstm_m[l].astype(jnp.float32), cache_pool[l], PAST_LEN, *lw)
        Cp_l.append(Cp); np_l.append(np_); mp_l.append(mp); bp_l.append(bp)
        Cs_l.append(Cs); ns_l.append(ns); ms_l.append(ms); bs_l.append(bs)
    y_prompt = rmsnorm(xp, norm_f_g)
    y_sample = rmsnorm(xs, norm_f_g)
    return (y_prompt, y_sample,
            jnp.stack(Cp_l), jnp.stack(np_l), jnp.stack(mp_l), jnp.stack(bp_l),
            jnp.stack(Cs_l), jnp.stack(ns_l), jnp.stack(ms_l), jnp.stack(bs_l))
```

```python
import functools

import jax
import jax.numpy as jnp
from jax import lax
from jax.experimental import pallas as pl
from jax.experimental.pallas import tpu as pltpu

F32 = jnp.float32
BF16 = jnp.bfloat16

D_MODEL = 1024
N_HEADS = 4
DH = 128
D_MLSTM = N_HEADS * DH
D_POOL = 512
POOL_WINDOWS = (2, 4, 8, 16)
POOL_STATE = 15
POOL_ROWS = 16
N_GROUPS = 4
EXPERTS_PER_GROUP = 4
N_EXPERTS = 16
D_EXPERT = 512
EPS = 1e-6
M_INIT = -1e30
PAST_LEN = 4096

LANE = 128
CHUNK = 128
LOOKBACK = 128
GROUP_LANE0 = 16
VMEM_LIMIT = 56 * 1024 * 1024


def _rms(x):
    return x * lax.rsqrt(jnp.mean(x * x, axis=-1, keepdims=True) + EPS)


def _log_sigmoid(z):
    return jnp.minimum(z, 0.0) - jnp.log1p(jnp.exp(-jnp.abs(z)))


def _inproj_kernel(x_ref, g1_ref, wqkvo_ref, wu_ref, wgt_ref, bgt_ref, poolw_ref, pscale_ref, cache_ref,
                   q_ref, k_ref, v_ref, og_ref, gates_ref, hp_ref, pstate_ref,
                   bandm_ref, bandp_ref, prev_ref, *, tm, n_seq, seq_tiles, pos0, ls_out):
    i = pl.program_id(0)
    ls = tm // n_seq
    lp = LOOKBACK // n_seq
    t_in_seq = i % seq_tiles

    @pl.when(i == 0)
    def _():
        row = lax.broadcasted_iota(jnp.int32, (tm, tm), 0)
        col = lax.broadcasted_iota(jnp.int32, (tm, tm), 1)
        same = (row // ls) == (col // ls)
        dist = (row % ls) - (col % ls)
        rowp = lax.broadcasted_iota(jnp.int32, (tm, LOOKBACK), 0)
        colp = lax.broadcasted_iota(jnp.int32, (tm, LOOKBACK), 1)
        samep = (rowp // ls) == (colp // lp)
        distp = (rowp % ls) - (colp % lp) + lp
        for g, w in enumerate(POOL_WINDOWS):
            bandm_ref[g] = jnp.where(same & (dist >= 0) & (dist < w), 1.0, 0.0).astype(BF16)
            bandp_ref[g] = jnp.where(samep & (distp < w), 1.0, 0.0).astype(BF16)

    @pl.when(t_in_seq == 0)
    def _():
        prev_ref[...] = jnp.zeros_like(prev_ref)
        for s in range(n_seq):
            prev_ref[(s + 1) * lp - POOL_ROWS:(s + 1) * lp, :] = cache_ref[s].astype(BF16)

    x = x_ref[...]
    hb = (_rms(x) * g1_ref[...]).astype(BF16)
    p = jnp.dot(hb, wqkvo_ref[...], preferred_element_type=F32)
    u = jnp.dot(hb, wu_ref[...], preferred_element_type=F32)
    z = jnp.dot(hb, wgt_ref[...], preferred_element_type=F32) + bgt_ref[...]
    lane = lax.broadcasted_iota(jnp.int32, (tm, LANE), 1)
    gates = jnp.where(lane < N_HEADS, z, jnp.where(lane < 2 * N_HEADS, _log_sigmoid(z), 0.0))

    qv = p[:, 0:D_MLSTM].astype(BF16)
    kv = (p[:, D_MLSTM:2 * D_MLSTM] * (DH ** -0.5)).astype(BF16)
    vv = p[:, 2 * D_MLSTM:3 * D_MLSTM].astype(BF16)
    ogv = jax.nn.sigmoid(p[:, 3 * D_MLSTM:4 * D_MLSTM]).astype(BF16)

    if ls_out == ls:
        q_ref[...] = qv
        k_ref[...] = kv
        v_ref[...] = vv
        og_ref[...] = ogv
        gates_ref[...] = gates
    else:
        lane_o = lax.broadcasted_iota(jnp.int32, (ls_out - ls, LANE), 1)
        gpad = jnp.where(lane_o < N_HEADS, M_INIT, 0.0).astype(F32)
        zpad = jnp.zeros((ls_out - ls, D_MLSTM), BF16)
        for s in range(n_seq):
            src = slice(s * ls, (s + 1) * ls)
            dst = slice(s * ls_out, s * ls_out + ls)
            pad = slice(s * ls_out + ls, (s + 1) * ls_out)
            for ref, val in ((q_ref, qv), (k_ref, kv), (v_ref, vv), (og_ref, ogv)):
                ref[dst, :] = val[src, :]
                ref[pad, :] = zpad
            gates_ref[dst, :] = gates[src, :]
            gates_ref[pad, :] = gpad

    ub = u.astype(BF16)
    prev = prev_ref[...]
    rowi = lax.broadcasted_iota(jnp.int32, (tm, 1), 0)
    pos = pos0 + t_in_seq * tm + rowi % ls
    outs = []
    for g, w in enumerate(POOL_WINDOWS):
        cs = slice(g * LANE, (g + 1) * LANE)
        ssum = (jnp.dot(bandm_ref[g], ub[:, cs], preferred_element_type=F32)
                + jnp.dot(bandp_ref[g], prev[:, cs], preferred_element_type=F32))
        cnt = jnp.minimum(pos + 1, w).astype(F32)
        dlt = ssum / cnt - u[:, cs]
        outs.append(jnp.dot(dlt.astype(BF16), poolw_ref[g], preferred_element_type=F32))
    hp = jnp.concatenate(outs, axis=-1) * pscale_ref[...]
    hp_ref[...] = hp.astype(BF16)

    if n_seq == 1:
        prev_ref[...] = ub[tm - LOOKBACK:, :]

        @pl.when(t_in_seq == seq_tiles - 1)
        def _():
            pstate_ref[0] = u[tm - POOL_ROWS:, :]
    else:
        for s in range(n_seq):
            pstate_ref[s] = u[(s + 1) * ls - POOL_ROWS:(s + 1) * ls, :]


def _inproj(x, cache16, g1, wqkvo, wu, wgt, bgt, poolw, pscale, *, seq_len, pos0):
    t_total = x.shape[0]
    n_b = t_total // seq_len
    if seq_len >= 512:
        tm, n_seq, seq_tiles, ls_out = 512, 1, seq_len // 512, 512
    else:
        n_seq = LOOKBACK // seq_len
        tm, seq_tiles, ls_out = LOOKBACK, 1, CHUNK
    assert t_total % tm == 0 and seq_len % (tm // n_seq) == 0 and seq_len >= POOL_ROWS
    n_tiles = t_total // tm
    tm_out = n_seq * ls_out
    t_out = n_tiles * tm_out
    const = lambda i: (0, 0)
    kern = functools.partial(_inproj_kernel, tm=tm, n_seq=n_seq, seq_tiles=seq_tiles, pos0=pos0, ls_out=ls_out)
    big = jax.ShapeDtypeStruct((t_out, D_MLSTM), BF16)
    return pl.pallas_call(
        kern,
        grid=(n_tiles,),
        in_specs=[
            pl.BlockSpec((tm, D_MODEL), lambda i: (i, 0)),
            pl.BlockSpec((1, D_MODEL), const),
            pl.BlockSpec((D_MODEL, 4 * D_MLSTM), const),
            pl.BlockSpec((D_MODEL, D_POOL), const),
            pl.BlockSpec((D_MODEL, LANE), const),
            pl.BlockSpec((1, LANE), const),
            pl.BlockSpec((len(POOL_WINDOWS), LANE, LANE), lambda i: (0, 0, 0)),
            pl.BlockSpec((1, D_POOL), const),
            pl.BlockSpec((n_seq, POOL_ROWS, D_POOL), lambda i: ((i // seq_tiles), 0, 0)),
        ],
        out_specs=[
            pl.BlockSpec((tm_out, D_MLSTM), lambda i: (i, 0)),
            pl.BlockSpec((tm_out, D_MLSTM), lambda i: (i, 0)),
            pl.BlockSpec((tm_out, D_MLSTM), lambda i: (i, 0)),
            pl.BlockSpec((tm_out, D_MLSTM), lambda i: (i, 0)),
            pl.BlockSpec((tm_out, LANE), lambda i: (i, 0)),
            pl.BlockSpec((tm, D_POOL), lambda i: (i, 0)),
            pl.BlockSpec((n_seq, POOL_ROWS, D_POOL), lambda i: ((i // seq_tiles), 0, 0)),
        ],
        out_shape=[big, big, big, big,
                   jax.ShapeDtypeStruct((t_out, LANE), F32),
                   jax.ShapeDtypeStruct((t_total, D_POOL), BF16),
                   jax.ShapeDtypeStruct((n_b, POOL_ROWS, D_POOL), F32)],
        scratch_shapes=[
            pltpu.VMEM((len(POOL_WINDOWS), tm, tm), BF16),
            pltpu.VMEM((len(POOL_WINDOWS), tm, LOOKBACK), BF16),
            pltpu.VMEM((LOOKBACK, D_POOL), BF16),
        ],
        compiler_params=pltpu.CompilerParams(dimension_semantics=("arbitrary",), vmem_limit_bytes=VMEM_LIMIT),
        name="inproj",
    )(x, g1, wqkvo, wu, wgt, bgt, poolw, pscale, cache16)


def _split3(x):
    hi = x.astype(BF16)
    r1 = x - hi.astype(F32)
    mid = r1.astype(BF16)
    lo = (r1 - mid.astype(F32)).astype(BF16)
    return hi, mid, lo


def _mlstm_kernel(q_ref, k_ref, v_ref, og_ref, gates_ref, mhg_ref, cn0_ref, m0_ref,
                  hm_ref, cn_out_ref, m_out_ref, cn_s, m_s, *, n_chunks):
    j = pl.program_id(1)

    @pl.when(j == 0)
    def _():
        cn_s[...] = cn0_ref[0]
        m_s[...] = m0_ref[0]

    row = lax.broadcasted_iota(jnp.int32, (CHUNK, CHUNK), 0)
    col = lax.broadcasted_iota(jnp.int32, (CHUNK, CHUNK), 1)
    causal = col <= row
    tri = jnp.where(causal, 1.0, 0.0).astype(BF16)
    ones = jnp.ones((CHUNK, DH), BF16)
    mhg = mhg_ref[...]

    cn = [cn_s[h] for h in range(N_HEADS)]
    m_prev = [m_s[h][0:1, 0:1] for h in range(N_HEADS)]

    for c in range(n_chunks):
        rs = slice(c * CHUNK, (c + 1) * CHUNK)
        gc = gates_ref[rs, :]
        hi, mid, lo = _split3(gc)
        bcum = (jnp.dot(tri, hi, preferred_element_type=F32) + jnp.dot(tri, mid, preferred_element_type=F32)
                + jnp.dot(tri, lo, preferred_element_type=F32))
        r_all = gc - pltpu.roll(bcum, LANE - N_HEADS, 1)
        r_t = r_all.T
        for h in range(N_HEADS):
            hs = slice(h * DH, (h + 1) * DH)
            qh = q_ref[rs, hs]
            kh = k_ref[rs, hs]
            v_ext = jnp.concatenate([v_ref[rs, hs], ones], axis=-1)
            b_col = bcum[:, N_HEADS + h:N_HEADS + h + 1]
            gi_col = gc[:, h:h + 1]
            logd = jnp.where(causal, b_col + r_t[h:h + 1, :], -jnp.inf)
            m_t = jnp.maximum(b_col + m_prev[h], jnp.max(logd, axis=1, keepdims=True))
            d = jnp.exp(logd - m_t)
            inter = jnp.exp(b_col + m_prev[h] - m_t)
            qk = lax.dot_general(qh, kh, (((1,), (1,)), ((), ())), preferred_element_type=F32)
            s = (qk * d).astype(BF16)
            sv = jnp.dot(s, v_ext, preferred_element_type=F32)
            qcn = jnp.dot(qh, cn[h].astype(BF16), preferred_element_type=F32)
            num = inter * qcn[:, :DH] + sv[:, :DH]
            den = inter * qcn[:, DH:] + sv[:, DH:]
            hv = num / jnp.maximum(jnp.abs(den), jnp.exp(-m_t))
            hn = _rms(hv) * mhg[:, hs] * og_ref[rs, hs].astype(F32)
            hm_ref[rs, hs] = hn.astype(BF16)
            m_new = m_t[CHUNK - 1:CHUNK, :]
            b_last = b_col[CHUNK - 1:CHUNK, :]
            w_col = jnp.exp(b_last - b_col + gi_col - m_new)
            decay = jnp.exp(b_last + m_prev[h] - m_new)
            kw = (kh.astype(F32) * w_col).astype(BF16)
            cn[h] = decay * cn[h] + lax.dot_general(kw, v_ext, (((0,), (0,)), ((), ())),
                                                    preferred_element_type=F32)
            m_prev[h] = m_new

    for h in range(N_HEADS):
        cn_s[h] = cn[h]
        m_s[h] = jnp.broadcast_to(m_prev[h], (8, LANE))

    @pl.when(j == pl.num_programs(1) - 1)
    def _():
        for h in range(N_HEADS):
            cn_out_ref[0, h] = cn[h]
            m_out_ref[0, h] = jnp.broadcast_to(m_prev[h], (8, LANE))


def _mlstm(q, k, v, og, gates, mhg, cn0, m0, *, seq_len):
    n_b = q.shape[0] // seq_len
    lb = min(seq_len, 4 * CHUNK)
    assert seq_len % lb == 0 and lb % CHUNK == 0
    nb = seq_len // lb
    tok = lambda b, j: (b * nb + j, 0)
    st = lambda b, j: (b, 0, 0, 0)
    return pl.pallas_call(
        functools.partial(_mlstm_kernel, n_chunks=lb // CHUNK),
        grid=(n_b, nb),
        in_specs=[
            pl.BlockSpec((lb, D_MLSTM), tok),
            pl.BlockSpec((lb, D_MLSTM), tok),
            pl.BlockSpec((lb, D_MLSTM), tok),
            pl.BlockSpec((lb, D_MLSTM), tok),
            pl.BlockSpec((lb, LANE), tok),
            pl.BlockSpec((1, D_MLSTM), lambda b, j: (0, 0)),
            pl.BlockSpec((1, N_HEADS, DH, 2 * DH), st),
            pl.BlockSpec((1, N_HEADS, 8, LANE), st),
        ],
        out_specs=[
            pl.BlockSpec((lb, D_MLSTM), tok),
            pl.BlockSpec((1, N_HEADS, DH, 2 * DH), st),
            pl.BlockSpec((1, N_HEADS, 8, LANE), st),
        ],
        out_shape=[
            jax.ShapeDtypeStruct((n_b * seq_len, D_MLSTM), BF16),
            jax.ShapeDtypeStruct((n_b, N_HEADS, DH, 2 * DH), F32),
            jax.ShapeDtypeStruct((n_b, N_HEADS, 8, LANE), F32),
        ],
        scratch_shapes=[
            pltpu.VMEM((N_HEADS, DH, 2 * DH), F32),
            pltpu.VMEM((N_HEADS, 8, LANE), F32),
        ],
        compiler_params=pltpu.CompilerParams(dimension_semantics=("arbitrary", "arbitrary"),
                                             vmem_limit_bytes=VMEM_LIMIT),
        name="mlstm",
    )(q, k, v, og, gates, mhg, cn0, m0)


def _route(logits):
    tm = logits.shape[0]
    lane = lax.broadcasted_iota(jnp.int32, (tm, LANE), 1).astype(F32)
    ninf = -jnp.inf
    big = float(LANE)

    def first_argmax(vals):
        mx = jnp.max(vals, axis=1, keepdims=True)
        idx = jnp.min(jnp.where(vals == mx, lane, big), axis=1, keepdims=True)
        return mx, idx

    lg = jnp.where((lane >= GROUP_LANE0) & (lane < GROUP_LANE0 + N_GROUPS), logits, ninf)
    gmax, glane = first_argmax(lg)
    pg = 1.0 / jnp.sum(jnp.exp(lg - gmax), axis=1, keepdims=True)
    lo = (glane - GROUP_LANE0) * EXPERTS_PER_GROUP
    le = jnp.where((lane >= lo) & (lane < lo + EXPERTS_PER_GROUP), logits, ninf)
    m1, i1 = first_argmax(le)
    le2 = jnp.where(lane == i1, ninf, le)
    m2, i2 = first_argmax(le2)
    e = jnp.exp(m2 - m1)
    ga = pg / (1.0 + e)
    gb = pg * e / (1.0 + e)
    return jnp.where(lane == i1, ga, 0.0) + jnp.where(lane == i2, gb, 0.0)


def _outproj_kernel(hm_ref, hp_ref, x_ref, wo1_ref, wo2_ref, g2_ref, wr_ref, br_ref,
                    x2_ref, xn_ref, gates_ref):
    x2 = (x_ref[...] + jnp.dot(hm_ref[...], wo1_ref[...], preferred_element_type=F32)
          + jnp.dot(hp_ref[...], wo2_ref[...], preferred_element_type=F32))
    x2_ref[...] = x2
    xn = (_rms(x2) * g2_ref[...]).astype(BF16)
    xn_ref[...] = xn
    logits = jnp.dot(xn, wr_ref[...], preferred_element_type=F32) + br_ref[...]
    gates_ref[...] = _route(logits)


def _outproj(hm, hp, x, wo1, wo2, g2, wr, br, *, tm, hm_stride):
    t_total = x.shape[0]
    assert t_total % tm == 0
    const = lambda i: (0, 0)
    return pl.pallas_call(
        _outproj_kernel,
        grid=(t_total // tm,),
        in_specs=[
            pl.BlockSpec((tm, D_MLSTM), lambda i: (i * hm_stride, 0)),
            pl.BlockSpec((tm, D_POOL), lambda i: (i, 0)),
            pl.BlockSpec((tm, D_MODEL), lambda i: (i, 0)),
            pl.BlockSpec((D_MLSTM, D_MODEL), const),
            pl.BlockSpec((D_POOL, D_MODEL), const),
            pl.BlockSpec((1, D_MODEL), const),
            pl.BlockSpec((D_MODEL, LANE), const),
            pl.BlockSpec((1, LANE), const),
        ],
        out_specs=[
            pl.BlockSpec((tm, D_MODEL), lambda i: (i, 0)),
            pl.BlockSpec((tm, D_MODEL), lambda i: (i, 0)),
            pl.BlockSpec((tm, LANE), lambda i: (i, 0)),
        ],
        out_shape=[
            jax.ShapeDtypeStruct((t_total, D_MODEL), F32),
            jax.ShapeDtypeStruct((t_total, D_MODEL), BF16),
            jax.ShapeDtypeStruct((t_total, LANE), F32),
        ],
        compiler_params=pltpu.CompilerParams(dimension_semantics=("arbitrary",), vmem_limit_bytes=VMEM_LIMIT),
        name="outproj",
    )(hm, hp, x, wo1, wo2, g2, wr, br)


def _moe_kernel(xn_ref, x2_ref, gates_ref, wg_ref, wu_ref, wd_ref, gf_ref, y_ref, acc_ref):
    e = pl.program_id(1)

    @pl.when(e == 0)
    def _():
        acc_ref[...] = jnp.zeros_like(acc_ref)

    xn = xn_ref[...]
    a = jnp.dot(xn, wg_ref[0], preferred_element_type=F32)
    b = jnp.dot(xn, wu_ref[0], preferred_element_type=F32)
    hid = (a * jax.nn.sigmoid(a) * b).astype(BF16)
    ye = jnp.dot(hid, wd_ref[0], preferred_element_type=F32)
    lane = lax.broadcasted_iota(jnp.int32, gates_ref.shape, 1)
    gcol = jnp.sum(jnp.where(lane == e, gates_ref[...], 0.0), axis=1, keepdims=True)
    acc_ref[...] += gcol * ye

    @pl.when(e == N_EXPERTS - 1)
    def _():
        y_ref[...] = _rms(x2_ref[...] + acc_ref[...]) * gf_ref[...]


def _moe(xn, x2, gates, wg, wu, wd, gf, *, tm):
    t_total = xn.shape[0]
    assert t_total % tm == 0
    return pl.pallas_call(
        _moe_kernel,
        grid=(t_total // tm, N_EXPERTS),
        in_specs=[
            pl.BlockSpec((tm, D_MODEL), lambda i, e: (i, 0)),
            pl.BlockSpec((tm, D_MODEL), lambda i, e: (i, 0)),
            pl.BlockSpec((tm, LANE), lambda i, e: (i, 0)),
            pl.BlockSpec((1, D_MODEL, D_EXPERT), lambda i, e: (e, 0, 0)),
            pl.BlockSpec((1, D_MODEL, D_EXPERT), lambda i, e: (e, 0, 0)),
            pl.BlockSpec((1, D_EXPERT, D_MODEL), lambda i, e: (e, 0, 0)),
            pl.BlockSpec((1, D_MODEL), lambda i, e: (0, 0)),
        ],
        out_specs=pl.BlockSpec((tm, D_MODEL), lambda i, e: (i, 0)),
        out_shape=jax.ShapeDtypeStruct((t_total, D_MODEL), F32),
        scratch_shapes=[pltpu.VMEM((tm, D_MODEL), F32)],
        compiler_params=pltpu.CompilerParams(dimension_semantics=("arbitrary", "arbitrary"),
                                             vmem_limit_bytes=VMEM_LIMIT),
        name="moe",
    )(xn, x2, gates, wg, wu, wd, gf)


def _stream(x, c0, n0, m0, cache, pos0, wts):
    (g1, wqkvo, wu, wgt, bgt, poolw, pscale, mhg, wo1, wo2, g2, wr, br, wg, wup, wd, gf) = wts
    n_b, seq_len, _ = x.shape
    xf = x.reshape(n_b * seq_len, D_MODEL)
    cache16 = jnp.pad(cache.astype(F32), ((0, 0), (POOL_ROWS - POOL_STATE, 0), (0, 0)))
    q, k, v, og, gates, hp, pstate = _inproj(xf, cache16, g1, wqkvo, wu, wgt, bgt, poolw, pscale,
                                             seq_len=seq_len, pos0=pos0)
    slot = q.shape[0] // n_b
    cn0 = jnp.concatenate([c0, jnp.broadcast_to(n0[..., None], c0.shape)], axis=-1)
    m0r = jnp.broadcast_to(m0[..., None, None], (n_b, N_HEADS, 8, LANE))
    hm, cn, mo = _mlstm(q, k, v, og, gates, mhg, cn0, m0r, seq_len=slot)
    tm = min(seq_len, 512)
    x2, xn, rg = _outproj(hm, hp, xf, wo1, wo2, g2, wr, br, tm=tm, hm_stride=slot // seq_len)
    y = _moe(xn, x2, rg, wg, wup, wd, gf, tm=min(n_b * seq_len, 512))
    return (y.reshape(n_b, seq_len, D_MODEL), cn[..., :DH], cn[..., DH], mo[:, :, 0, 0],
            pstate[:, POOL_ROWS - POOL_STATE:, :])


def kernel(x_prompt, x_sample, state_mlstm_C, state_mlstm_n, state_mlstm_m, cache_pool, norm1_g, w_in, b_gates,
           mh_norm_g, pool_w, pool_scale, w_out, norm2_g, w_router_group, b_router_group, w_router_expert,
           b_router_expert, w_gate, w_up, w_down, norm_f_g):
    depth = w_in.shape[0]
    assert depth == 1 and x_sample.shape[1] * 4 == LOOKBACK
    n_bp = x_prompt.shape[0]
    xp, xs = x_prompt, x_sample
    outs_p, outs_s = [], []
    for l in range(depth):
        w = w_in[l]
        ng = 2 * N_HEADS
        c_u = 4 * D_MLSTM + ng
        wgt = jnp.pad(w[:, 4 * D_MLSTM:c_u], ((0, 0), (0, LANE - ng))).astype(BF16)
        bgt = jnp.pad(b_gates[l].astype(F32), (0, LANE - ng))[None, :]
        wr = jnp.pad(jnp.concatenate([w_router_expert[l], w_router_group[l]], axis=1),
                     ((0, 0), (0, LANE - N_EXPERTS - N_GROUPS))).astype(BF16)
        br = jnp.pad(jnp.concatenate([b_router_expert[l], b_router_group[l]]).astype(F32),
                     (0, LANE - N_EXPERTS - N_GROUPS))[None, :]
        wts = (norm1_g[l][None, :], w[:, :4 * D_MLSTM].astype(BF16), w[:, c_u:].astype(BF16), wgt, bgt,
               pool_w[l].astype(BF16), pool_scale[l][None, :].astype(F32), mh_norm_g[l][None, :].astype(F32),
               w_out[l][:D_MLSTM].astype(BF16), w_out[l][D_MLSTM:].astype(BF16), norm2_g[l][None, :], wr, br,
               w_gate[l].astype(BF16), w_up[l].astype(BF16), w_down[l].astype(BF16), norm_f_g[None, :])
        c0 = jnp.zeros((n_bp, N_HEADS, DH, DH), F32)
        n0 = jnp.zeros((n_bp, N_HEADS, DH), F32)
        m0 = jnp.full((n_bp, N_HEADS), M_INIT, F32)
        buf0 = jnp.zeros((n_bp, POOL_STATE, D_POOL), F32)
        res_p = _stream(xp, c0, n0, m0, buf0, 0, wts)
        res_s = _stream(xs, state_mlstm_C[l].astype(F32), state_mlstm_n[l].astype(F32),
                        state_mlstm_m[l].astype(F32), cache_pool[l], PAST_LEN, wts)
        xp, xs = res_p[0], res_s[0]
        outs_p.append(res_p[1:])
        outs_s.append(res_s[1:])
    stack = lambda outs, i: jnp.stack([o[i] for o in outs])
    return (xp, xs,
            stack(outs_p, 0), stack(outs_p, 1), stack(outs_p, 2), stack(outs_p, 3),
            stack(outs_s, 0), stack(outs_s, 1), stack(outs_s, 2), stack(outs_s, 3))
```

```python
import functools

import jax
import jax.numpy as jnp
from jax import lax
from jax.experimental import pallas as pl
from jax.experimental.pallas import tpu as pltpu

F32 = jnp.float32
BF16 = jnp.bfloat16

D_MODEL = 1024
N_HEADS = 4
DH = 128
D_MLSTM = N_HEADS * DH
D_POOL = 512
POOL_WINDOWS = (2, 4, 8, 16)
POOL_STATE = 15
POOL_ROWS = 16
N_GROUPS = 4
EXPERTS_PER_GROUP = 4
N_EXPERTS = 16
N_PAIRS = 6
N_BINS = N_GROUPS * N_PAIRS
D_EXPERT = 512
EPS = 1e-6
M_INIT = -1e30
PAST_LEN = 4096

LANE = 128
CHUNK = 128
LOOKBACK = 128
GROUP_LANE0 = 16
VMEM_LIMIT = 56 * 1024 * 1024


def _rms(x):
    return x * lax.rsqrt(jnp.mean(x * x, axis=-1, keepdims=True) + EPS)


def _log_sigmoid(z):
    return jnp.minimum(z, 0.0) - jnp.log1p(jnp.exp(-jnp.abs(z)))


def _inproj_kernel(x_ref, g1_ref, wqkvo_ref, wu_ref, wgt_ref, bgt_ref, poolw_ref, pscale_ref, cache_ref,
                   q_ref, k_ref, v_ref, og_ref, gates_ref, hp_ref, pstate_ref,
                   bandm_ref, bandp_ref, prev_ref, *, tm, n_seq, seq_tiles, pos0, ls_out):
    i = pl.program_id(0)
    ls = tm // n_seq
    lp = LOOKBACK // n_seq
    t_in_seq = i % seq_tiles

    @pl.when(i == 0)
    def _():
        row = lax.broadcasted_iota(jnp.int32, (tm, tm), 0)
        col = lax.broadcasted_iota(jnp.int32, (tm, tm), 1)
        same = (row // ls) == (col // ls)
        dist = (row % ls) - (col % ls)
        rowp = lax.broadcasted_iota(jnp.int32, (tm, LOOKBACK), 0)
        colp = lax.broadcasted_iota(jnp.int32, (tm, LOOKBACK), 1)
        samep = (rowp // ls) == (colp // lp)
        distp = (rowp % ls) - (colp % lp) + lp
        for g, w in enumerate(POOL_WINDOWS):
            bandm_ref[g] = jnp.where(same & (dist >= 0) & (dist < w), 1.0, 0.0).astype(BF16)
            bandp_ref[g] = jnp.where(samep & (distp < w), 1.0, 0.0).astype(BF16)

    @pl.when(t_in_seq == 0)
    def _():
        prev_ref[...] = jnp.zeros_like(prev_ref)
        for s in range(n_seq):
            prev_ref[(s + 1) * lp - POOL_ROWS:(s + 1) * lp, :] = cache_ref[s].astype(BF16)

    x = x_ref[...]
    hb = (_rms(x) * g1_ref[...]).astype(BF16)
    p = jnp.dot(hb, wqkvo_ref[...], preferred_element_type=F32)
    u = jnp.dot(hb, wu_ref[...], preferred_element_type=F32)
    z = jnp.dot(hb, wgt_ref[...], preferred_element_type=F32) + bgt_ref[...]
    lane = lax.broadcasted_iota(jnp.int32, (tm, LANE), 1)
    gates = jnp.where(lane < N_HEADS, z, jnp.where(lane < 2 * N_HEADS, _log_sigmoid(z), 0.0))

    qv = p[:, 0:D_MLSTM].astype(BF16)
    kv = (p[:, D_MLSTM:2 * D_MLSTM] * (DH ** -0.5)).astype(BF16)
    vv = p[:, 2 * D_MLSTM:3 * D_MLSTM].astype(BF16)
    ogv = jax.nn.sigmoid(p[:, 3 * D_MLSTM:4 * D_MLSTM]).astype(BF16)

    if ls_out == ls:
        q_ref[...] = qv
        k_ref[...] = kv
        v_ref[...] = vv
        og_ref[...] = ogv
        gates_ref[...] = gates
    else:
        lane_o = lax.broadcasted_iota(jnp.int32, (ls_out - ls, LANE), 1)
        gpad = jnp.where(lane_o < N_HEADS, M_INIT, 0.0).astype(F32)
        zpad = jnp.zeros((ls_out - ls, D_MLSTM), BF16)
        for s in range(n_seq):
            src = slice(s * ls, (s + 1) * ls)
            dst = slice(s * ls_out, s * ls_out + ls)
            pad = slice(s * ls_out + ls, (s + 1) * ls_out)
            for ref, val in ((q_ref, qv), (k_ref, kv), (v_ref, vv), (og_ref, ogv)):
                ref[dst, :] = val[src, :]
                ref[pad, :] = zpad
            gates_ref[dst, :] = gates[src, :]
            gates_ref[pad, :] = gpad

    ub = u.astype(BF16)
    prev = prev_ref[...]
    rowi = lax.broadcasted_iota(jnp.int32, (tm, 1), 0)
    pos = pos0 + t_in_seq * tm + rowi % ls
    outs = []
    for g, w in enumerate(POOL_WINDOWS):
        cs = slice(g * LANE, (g + 1) * LANE)
        ssum = (jnp.dot(bandm_ref[g], ub[:, cs], preferred_element_type=F32)
                + jnp.dot(bandp_ref[g], prev[:, cs], preferred_element_type=F32))
        cnt = jnp.minimum(pos + 1, w).astype(F32)
        dlt = ssum / cnt - u[:, cs]
        outs.append(jnp.dot(dlt.astype(BF16), poolw_ref[g], preferred_element_type=F32))
    hp = jnp.concatenate(outs, axis=-1) * pscale_ref[...]
    hp_ref[...] = hp.astype(BF16)

    if n_seq == 1:
        prev_ref[...] = ub[tm - LOOKBACK:, :]

        @pl.when(t_in_seq == seq_tiles - 1)
        def _():
            pstate_ref[0] = u[tm - POOL_ROWS:, :]
    else:
        for s in range(n_seq):
            pstate_ref[s] = u[(s + 1) * ls - POOL_ROWS:(s + 1) * ls, :]


def _inproj(x, cache16, g1, wqkvo, wu, wgt, bgt, poolw, pscale, *, seq_len, pos0):
    t_total = x.shape[0]
    n_b = t_total // seq_len
    if seq_len >= 512:
        tm, n_seq, seq_tiles, ls_out = 512, 1, seq_len // 512, 512
    else:
        n_seq = LOOKBACK // seq_len
        tm, seq_tiles, ls_out = LOOKBACK, 1, CHUNK
    assert t_total % tm == 0 and seq_len % (tm // n_seq) == 0 and seq_len >= POOL_ROWS
    n_tiles = t_total // tm
    tm_out = n_seq * ls_out
    t_out = n_tiles * tm_out
    const = lambda i: (0, 0)
    kern = functools.partial(_inproj_kernel, tm=tm, n_seq=n_seq, seq_tiles=seq_tiles, pos0=pos0, ls_out=ls_out)
    big = jax.ShapeDtypeStruct((t_out, D_MLSTM), BF16)
    return pl.pallas_call(
        kern,
        grid=(n_tiles,),
        in_specs=[
            pl.BlockSpec((tm, D_MODEL), lambda i: (i, 0)),
            pl.BlockSpec((1, D_MODEL), const),
            pl.BlockSpec((D_MODEL, 4 * D_MLSTM), const),
            pl.BlockSpec((D_MODEL, D_POOL), const),
            pl.BlockSpec((D_MODEL, LANE), const),
            pl.BlockSpec((1, LANE), const),
            pl.BlockSpec((len(POOL_WINDOWS), LANE, LANE), lambda i: (0, 0, 0)),
            pl.BlockSpec((1, D_POOL), const),
            pl.BlockSpec((n_seq, POOL_ROWS, D_POOL), lambda i: ((i // seq_tiles), 0, 0)),
        ],
        out_specs=[
            pl.BlockSpec((tm_out, D_MLSTM), lambda i: (i, 0)),
            pl.BlockSpec((tm_out, D_MLSTM), lambda i: (i, 0)),
            pl.BlockSpec((tm_out, D_MLSTM), lambda i: (i, 0)),
            pl.BlockSpec((tm_out, D_MLSTM), lambda i: (i, 0)),
            pl.BlockSpec((tm_out, LANE), lambda i: (i, 0)),
            pl.BlockSpec((tm, D_POOL), lambda i: (i, 0)),
            pl.BlockSpec((n_seq, POOL_ROWS, D_POOL), lambda i: ((i // seq_tiles), 0, 0)),
        ],
        out_shape=[big, big, big, big,
                   jax.ShapeDtypeStruct((t_out, LANE), F32),
                   jax.ShapeDtypeStruct((t_total, D_POOL), BF16),
                   jax.ShapeDtypeStruct((n_b, POOL_ROWS, D_POOL), F32)],
        scratch_shapes=[
            pltpu.VMEM((len(POOL_WINDOWS), tm, tm), BF16),
            pltpu.VMEM((len(POOL_WINDOWS), tm, LOOKBACK), BF16),
            pltpu.VMEM((LOOKBACK, D_POOL), BF16),
        ],
        compiler_params=pltpu.CompilerParams(dimension_semantics=("arbitrary",), vmem_limit_bytes=VMEM_LIMIT),
        name="inproj",
    )(x, g1, wqkvo, wu, wgt, bgt, poolw, pscale, cache16)


def _split3(x):
    hi = x.astype(BF16)
    r1 = x - hi.astype(F32)
    mid = r1.astype(BF16)
    lo = (r1 - mid.astype(F32)).astype(BF16)
    return hi, mid, lo


def _mlstm_kernel(q_ref, k_ref, v_ref, og_ref, gates_ref, mhg_ref, cn0_ref, m0_ref,
                  hm_ref, cn_out_ref, m_out_ref, cn_s, m_s, *, n_chunks):
    j = pl.program_id(1)

    @pl.when(j == 0)
    def _():
        cn_s[...] = cn0_ref[0]
        m_s[...] = m0_ref[0]

    row = lax.broadcasted_iota(jnp.int32, (CHUNK, CHUNK), 0)
    col = lax.broadcasted_iota(jnp.int32, (CHUNK, CHUNK), 1)
    causal = col <= row
    tri = jnp.where(causal, 1.0, 0.0).astype(BF16)
    ones = jnp.ones((CHUNK, DH), BF16)
    mhg = mhg_ref[...]

    cn = [cn_s[h] for h in range(N_HEADS)]
    m_prev = [m_s[h][0:1, 0:1] for h in range(N_HEADS)]

    for c in range(n_chunks):
        rs = slice(c * CHUNK, (c + 1) * CHUNK)
        gc = gates_ref[rs, :]
        hi, mid, lo = _split3(gc)
        bcum = (jnp.dot(tri, hi, preferred_element_type=F32) + jnp.dot(tri, mid, preferred_element_type=F32)
                + jnp.dot(tri, lo, preferred_element_type=F32))
        r_all = gc - pltpu.roll(bcum, LANE - N_HEADS, 1)
        r_t = r_all.T
        for h in range(N_HEADS):
            hs = slice(h * DH, (h + 1) * DH)
            qh = q_ref[rs, hs]
            kh = k_ref[rs, hs]
            v_ext = jnp.concatenate([v_ref[rs, hs], ones], axis=-1)
            b_col = bcum[:, N_HEADS + h:N_HEADS + h + 1]
            gi_col = gc[:, h:h + 1]
            logd = jnp.where(causal, b_col + r_t[h:h + 1, :], -jnp.inf)
            m_t = jnp.maximum(b_col + m_prev[h], jnp.max(logd, axis=1, keepdims=True))
            d = jnp.exp(logd - m_t)
            inter = jnp.exp(b_col + m_prev[h] - m_t)
            qk = lax.dot_general(qh, kh, (((1,), (1,)), ((), ())), preferred_element_type=F32)
            s = (qk * d).astype(BF16)
            sv = jnp.dot(s, v_ext, preferred_element_type=F32)
            qcn = jnp.dot(qh, cn[h].astype(BF16), preferred_element_type=F32)
            num = inter * qcn[:, :DH] + sv[:, :DH]
            den = inter * qcn[:, DH:] + sv[:, DH:]
            hv = num / jnp.maximum(jnp.abs(den), jnp.exp(-m_t))
            hn = _rms(hv) * mhg[:, hs] * og_ref[rs, hs].astype(F32)
            hm_ref[rs, hs] = hn.astype(BF16)
            m_new = m_t[CHUNK - 1:CHUNK, :]
            b_last = b_col[CHUNK - 1:CHUNK, :]
            w_col = jnp.exp(b_last - b_col + gi_col - m_new)
            decay = jnp.exp(b_last + m_prev[h] - m_new)
            kw = (kh.astype(F32) * w_col).astype(BF16)
            cn[h] = decay * cn[h] + lax.dot_general(kw, v_ext, (((0,), (0,)), ((), ())),
                                                    preferred_element_type=F32)
            m_prev[h] = m_new

    for h in range(N_HEADS):
        cn_s[h] = cn[h]
        m_s[h] = jnp.broadcast_to(m_prev[h], (8, LANE))

    @pl.when(j == pl.num_programs(1) - 1)
    def _():
        for h in range(N_HEADS):
            cn_out_ref[0, h] = cn[h]
            m_out_ref[0, h] = jnp.broadcast_to(m_prev[h], (8, LANE))


def _mlstm(q, k, v, og, gates, mhg, cn0, m0, *, seq_len):
    n_b = q.shape[0] // seq_len
    lb = min(seq_len, 4 * CHUNK)
    assert seq_len % lb == 0 and lb % CHUNK == 0
    nb = seq_len // lb
    tok = lambda b, j: (b * nb + j, 0)
    st = lambda b, j: (b, 0, 0, 0)
    return pl.pallas_call(
        functools.partial(_mlstm_kernel, n_chunks=lb // CHUNK),
        grid=(n_b, nb),
        in_specs=[
            pl.BlockSpec((lb, D_MLSTM), tok),
            pl.BlockSpec((lb, D_MLSTM), tok),
            pl.BlockSpec((lb, D_MLSTM), tok),
            pl.BlockSpec((lb, D_MLSTM), tok),
            pl.BlockSpec((lb, LANE), tok),
            pl.BlockSpec((1, D_MLSTM), lambda b, j: (0, 0)),
            pl.BlockSpec((1, N_HEADS, DH, 2 * DH), st),
            pl.BlockSpec((1, N_HEADS, 8, LANE), st),
        ],
        out_specs=[
            pl.BlockSpec((lb, D_MLSTM), tok),
            pl.BlockSpec((1, N_HEADS, DH, 2 * DH), st),
            pl.BlockSpec((1, N_HEADS, 8, LANE), st),
        ],
        out_shape=[
            jax.ShapeDtypeStruct((n_b * seq_len, D_MLSTM), BF16),
            jax.ShapeDtypeStruct((n_b, N_HEADS, DH, 2 * DH), F32),
            jax.ShapeDtypeStruct((n_b, N_HEADS, 8, LANE), F32),
        ],
        scratch_shapes=[
            pltpu.VMEM((N_HEADS, DH, 2 * DH), F32),
            pltpu.VMEM((N_HEADS, 8, LANE), F32),
        ],
        compiler_params=pltpu.CompilerParams(dimension_semantics=("arbitrary", "arbitrary"),
                                             vmem_limit_bytes=VMEM_LIMIT),
        name="mlstm",
    )(q, k, v, og, gates, mhg, cn0, m0)


def _route(logits):
    tm = logits.shape[0]
    lane = lax.broadcasted_iota(jnp.int32, (tm, LANE), 1).astype(F32)
    ninf = -jnp.inf
    big = float(LANE)

    def first_argmax(vals):
        mx = jnp.max(vals, axis=1, keepdims=True)
        idx = jnp.min(jnp.where(vals == mx, lane, big), axis=1, keepdims=True)
        return mx, idx

    lg = jnp.where((lane >= GROUP_LANE0) & (lane < GROUP_LANE0 + N_GROUPS), logits, ninf)
    gmax, glane = first_argmax(lg)
    pg = 1.0 / jnp.sum(jnp.exp(lg - gmax), axis=1, keepdims=True)
    grp = glane - GROUP_LANE0
    lo = grp * EXPERTS_PER_GROUP
    le = jnp.where((lane >= lo) & (lane < lo + EXPERTS_PER_GROUP), logits, ninf)
    m1, i1 = first_argmax(le)
    le2 = jnp.where(lane == i1, ninf, le)
    m2, i2 = first_argmax(le2)
    e = jnp.exp(m2 - m1)
    ga = pg / (1.0 + e)
    gb = pg * e / (1.0 + e)
    dense = jnp.where(lane == i1, ga, 0.0) + jnp.where(lane == i2, gb, 0.0)
    a = jnp.minimum(i1, i2) - lo
    b = jnp.maximum(i1, i2) - lo
    pair = a * (2 * EXPERTS_PER_GROUP - 1 - a) * 0.5 + (b - a - 1.0)
    bin_id = grp * N_PAIRS + pair
    g_lo = jnp.where(i1 < i2, ga, gb)
    g_hi = jnp.where(i1 < i2, gb, ga)
    info = (jnp.where(lane == 0, g_lo, 0.0) + jnp.where(lane == 1, g_hi, 0.0) + jnp.where(lane == 2, bin_id, 0.0))
    return dense, info


def _outproj_kernel(hm_ref, hp_ref, x_ref, wo1_ref, wo2_ref, g2_ref, wr_ref, br_ref, *out_refs, sparse):
    x2 = (x_ref[...] + jnp.dot(hm_ref[...], wo1_ref[...], preferred_element_type=F32)
          + jnp.dot(hp_ref[...], wo2_ref[...], preferred_element_type=F32))
    xn = (_rms(x2) * g2_ref[...]).astype(BF16)
    logits = jnp.dot(xn, wr_ref[...], preferred_element_type=F32) + br_ref[...]
    dense, info = _route(logits)
    if sparse:
        x2e_ref, info_ref = out_refs
        x2e_ref[:, :D_MODEL] = x2
        x2e_ref[:, D_MODEL:] = info
        info_ref[...] = info
    else:
        x2_ref, xn_ref, gates_ref = out_refs
        x2_ref[...] = x2
        xn_ref[...] = xn
        gates_ref[...] = dense


def _outproj(hm, hp, x, wo1, wo2, g2, wr, br, *, tm, hm_stride, sparse):
    t_total = x.shape[0]
    assert t_total % tm == 0
    const = lambda i: (0, 0)
    row = lambda i: (i, 0)
    if sparse:
        out_specs = [pl.BlockSpec((tm, D_MODEL + LANE), row), pl.BlockSpec((tm, LANE), row)]
        out_shape = [jax.ShapeDtypeStruct((t_total, D_MODEL + LANE), F32),
                     jax.ShapeDtypeStruct((t_total, LANE), F32)]
    else:
        out_specs = [pl.BlockSpec((tm, D_MODEL), row), pl.BlockSpec((tm, D_MODEL), row),
                     pl.BlockSpec((tm, LANE), row)]
        out_shape = [jax.ShapeDtypeStruct((t_total, D_MODEL), F32),
                     jax.ShapeDtypeStruct((t_total, D_MODEL), BF16),
                     jax.ShapeDtypeStruct((t_total, LANE), F32)]
    return pl.pallas_call(
        functools.partial(_outproj_kernel, sparse=sparse),
        grid=(t_total // tm,),
        in_specs=[
            pl.BlockSpec((tm, D_MLSTM), lambda i: (i * hm_stride, 0)),
            pl.BlockSpec((tm, D_POOL), lambda i: (i, 0)),
            pl.BlockSpec((tm, D_MODEL), lambda i: (i, 0)),
            pl.BlockSpec((D_MLSTM, D_MODEL), const),
            pl.BlockSpec((D_POOL, D_MODEL), const),
            pl.BlockSpec((1, D_MODEL), const),
            pl.BlockSpec((D_MODEL, LANE), const),
            pl.BlockSpec((1, LANE), const),
        ],
        out_specs=out_specs,
        out_shape=out_shape,
        compiler_params=pltpu.CompilerParams(dimension_semantics=("arbitrary",), vmem_limit_bytes=VMEM_LIMIT),
        name="outproj",
    )(hm, hp, x, wo1, wo2, g2, wr, br)


def _moe_kernel(xn_ref, x2_ref, gates_ref, wg_ref, wu_ref, wd_ref, gf_ref, y_ref, acc_ref):
    e = pl.program_id(1)

    @pl.when(e == 0)
    def _():
        acc_ref[...] = jnp.zeros_like(acc_ref)

    xn = xn_ref[...]
    a = jnp.dot(xn, wg_ref[0], preferred_element_type=F32)
    b = jnp.dot(xn, wu_ref[0], preferred_element_type=F32)
    hid = (a * jax.nn.sigmoid(a) * b).astype(BF16)
    ye = jnp.dot(hid, wd_ref[0], preferred_element_type=F32)
    lane = lax.broadcasted_iota(jnp.int32, gates_ref.shape, 1)
    gcol = jnp.sum(jnp.where(lane == e, gates_ref[...], 0.0), axis=1, keepdims=True)
    acc_ref[...] += gcol * ye

    @pl.when(e == N_EXPERTS - 1)
    def _():
        y_ref[...] = _rms(x2_ref[...] + acc_ref[...]) * gf_ref[...]


def _moe(xn, x2, gates, wg, wu, wd, gf, *, tm):
    t_total = xn.shape[0]
    assert t_total % tm == 0
    return pl.pallas_call(
        _moe_kernel,
        grid=(t_total // tm, N_EXPERTS),
        in_specs=[
            pl.BlockSpec((tm, D_MODEL), lambda i, e: (i, 0)),
            pl.BlockSpec((tm, D_MODEL), lambda i, e: (i, 0)),
            pl.BlockSpec((tm, LANE), lambda i, e: (i, 0)),
            pl.BlockSpec((1, D_MODEL, D_EXPERT), lambda i, e: (e, 0, 0)),
            pl.BlockSpec((1, D_MODEL, D_EXPERT), lambda i, e: (e, 0, 0)),
            pl.BlockSpec((1, D_EXPERT, D_MODEL), lambda i, e: (e, 0, 0)),
            pl.BlockSpec((1, D_MODEL), lambda i, e: (0, 0)),
        ],
        out_specs=pl.BlockSpec((tm, D_MODEL), lambda i, e: (i, 0)),
        out_shape=jax.ShapeDtypeStruct((t_total, D_MODEL), F32),
        scratch_shapes=[pltpu.VMEM((tm, D_MODEL), F32)],
        compiler_params=pltpu.CompilerParams(dimension_semantics=("arbitrary", "arbitrary"),
                                             vmem_limit_bytes=VMEM_LIMIT),
        name="moe",
    )(xn, x2, gates, wg, wu, wd, gf)


def _moe_sparse_kernel(tile_ref, ea_ref, eb_ref, lo_ref, hi_ref, first_ref, last_ref, nvis_ref,
                       src_ref, src_next_ref, x2e_hbm, g2_ref, gf_ref,
                       wga_ref, wgb_ref, wua_ref, wub_ref, wda_ref, wdb_ref,
                       y_hbm, xbuf, ybuf, sem_in, sem_out, *, tm, n_tiles):
    v = pl.program_id(0)
    nvis = nvis_ref[0]
    t = tile_ref[v]
    slot = t % 2
    active = v < nvis
    first = jnp.logical_and(active, first_ref[v] == 1)
    last = jnp.logical_and(active, last_ref[v] == 1)

    def gather_copy(idx_ref, r, s):
        return pltpu.make_async_copy(x2e_hbm.at[pl.ds(idx_ref[r], 1), :], xbuf.at[s, pl.ds(r, 1), :], sem_in.at[s])

    def scatter_copy(r, s):
        return pltpu.make_async_copy(ybuf.at[s, pl.ds(r, 1), :], y_hbm.at[pl.ds(src_ref[r], 1), :], sem_out.at[s])

    def start_gather(idx_ref, s):
        def body(r, c):
            gather_copy(idx_ref, r, s).start()
            return c
        lax.fori_loop(0, tm, body, 0, unroll=8)

    def wait_gather(s):
        pltpu.make_async_copy(x2e_hbm.at[pl.ds(0, tm), :], xbuf.at[s], sem_in.at[s]).wait()

    def wait_scatter(s):
        pltpu.make_async_copy(ybuf.at[s], y_hbm.at[pl.ds(0, tm), :], sem_out.at[s]).wait()

    @pl.when(jnp.logical_and(first, v == 0))
    def _():
        start_gather(src_ref, slot)

    @pl.when(jnp.logical_and(first, t + 1 < n_tiles))
    def _():
        start_gather(src_next_ref, 1 - slot)

    @pl.when(first)
    def _():
        wait_gather(slot)

    @pl.when(jnp.logical_and(first, t >= 2))
    def _():
        wait_scatter(slot)

    @pl.when(active)
    def _():
        xb = xbuf[slot]
        x2 = xb[:, :D_MODEL]
        g_a = xb[:, D_MODEL:D_MODEL + 1]
        g_b = xb[:, D_MODEL + 1:D_MODEL + 2]
        xn = (_rms(x2) * g2_ref[...]).astype(BF16)

        def expert(wg, wu, wd):
            a = jnp.dot(xn, wg[0], preferred_element_type=F32)
            b = jnp.dot(xn, wu[0], preferred_element_type=F32)
            hid = (a * jax.nn.sigmoid(a) * b).astype(BF16)
            return jnp.dot(hid, wd[0], preferred_element_type=F32)

        moe = g_a * expert(wga_ref, wua_ref, wda_ref) + g_b * expert(wgb_ref, wub_ref, wdb_ref)
        out = _rms(x2 + moe) * gf_ref[...]
        rowi = t * tm + lax.broadcasted_iota(jnp.int32, (tm, 1), 0)
        mine = jnp.logical_and(rowi >= lo_ref[v], rowi < hi_ref[v])

        @pl.when(first_ref[v] == 1)
        def _():
            ybuf[slot] = out

        @pl.when(first_ref[v] == 0)
        def _():
            ybuf[slot] = jnp.where(mine, out, ybuf[slot])

    @pl.when(last)
    def _():
        def body(r, c):
            scatter_copy(r, slot).start()
            return c
        lax.fori_loop(0, tm, body, 0, unroll=8)

    @pl.when(jnp.logical_and(active, v == nvis - 1))
    def _():
        wait_scatter(slot)

        @pl.when(t >= 1)
        def _():
            wait_scatter(1 - slot)


def _visit_tables(bins, tm, n_vis):
    t_total = bins.shape[0]
    n_tiles = t_total // tm
    src = jnp.argsort(bins, stable=True).astype(jnp.int32)
    counts = jnp.sum((bins[:, None] == jnp.arange(N_BINS, dtype=jnp.int32)[None, :]).astype(jnp.int32), axis=0)
    off = jnp.cumsum(counts) - counts
    tile_lo = jnp.arange(n_tiles, dtype=jnp.int32)[:, None] * tm
    lo = jnp.maximum(off[None, :], tile_lo)
    hi = jnp.minimum((off + counts)[None, :], tile_lo + tm)
    ov = (lo < hi).ravel()
    nvis = jnp.sum(ov.astype(jnp.int32))
    idx = jnp.nonzero(ov, size=n_vis, fill_value=0)[0].astype(jnp.int32)
    vid = jnp.arange(n_vis, dtype=jnp.int32)
    idx = jnp.where(vid < nvis, idx, idx[jnp.maximum(nvis - 1, 0)])
    tile = idx // N_BINS
    b = idx % N_BINS
    grp, pair = b // N_PAIRS, b % N_PAIRS
    pa = jnp.array([0, 0, 0, 1, 1, 2], jnp.int32)[pair]
    pb = jnp.array([1, 2, 3, 2, 3, 3], jnp.int32)[pair]
    prev_tile = jnp.concatenate([jnp.full((1,), -1, jnp.int32), tile[:-1]])
    next_tile = jnp.concatenate([tile[1:], jnp.full((1,), -1, jnp.int32)])
    first = (tile != prev_tile).astype(jnp.int32)
    last = jnp.logical_or(tile != next_tile, vid == nvis - 1).astype(jnp.int32)
    return (src, tile, grp * EXPERTS_PER_GROUP + pa, grp * EXPERTS_PER_GROUP + pb,
            lo.ravel()[idx], hi.ravel()[idx], first, last, nvis[None])


def _moe_sparse(x2e, bins, g2, gf, wg, wu, wd, *, tm):
    t_total = x2e.shape[0]
    assert t_total % tm == 0
    n_tiles = t_total // tm
    n_vis = n_tiles + N_BINS - 1
    src, tile, ea, eb, lo, hi, first, last, nvis = _visit_tables(bins, tm, n_vis)
    const = lambda v, *_: (0, 0)
    w_a = lambda v, tile, ea, eb, *_: (ea[v], 0, 0)
    w_b = lambda v, tile, ea, eb, *_: (eb[v], 0, 0)
    grid_spec = pltpu.PrefetchScalarGridSpec(
        num_scalar_prefetch=8,
        grid=(n_vis,),
        in_specs=[
            pl.BlockSpec((tm,), lambda v, tile, *_: (tile[v],), memory_space=pltpu.SMEM),
            pl.BlockSpec((tm,), lambda v, tile, *_: (jnp.minimum(tile[v] + 1, n_tiles - 1),),
                         memory_space=pltpu.SMEM),
            pl.BlockSpec(memory_space=pl.ANY),
            pl.BlockSpec((1, D_MODEL), const),
            pl.BlockSpec((1, D_MODEL), const),
            pl.BlockSpec((1, D_MODEL, D_EXPERT), w_a),
            pl.BlockSpec((1, D_MODEL, D_EXPERT), w_b),
            pl.BlockSpec((1, D_MODEL, D_EXPERT), w_a),
            pl.BlockSpec((1, D_MODEL, D_EXPERT), w_b),
            pl.BlockSpec((1, D_EXPERT, D_MODEL), w_a),
            pl.BlockSpec((1, D_EXPERT, D_MODEL), w_b),
        ],
        out_specs=pl.BlockSpec(memory_space=pl.ANY),
        scratch_shapes=[
            pltpu.VMEM((2, tm, D_MODEL + LANE), F32),
            pltpu.VMEM((2, tm, D_MODEL), F32),
            pltpu.SemaphoreType.DMA((2,)),
            pltpu.SemaphoreType.DMA((2,)),
        ],
    )
    return pl.pallas_call(
        functools.partial(_moe_sparse_kernel, tm=tm, n_tiles=n_tiles),
        grid_spec=grid_spec,
        out_shape=jax.ShapeDtypeStruct((t_total, D_MODEL), F32),
        compiler_params=pltpu.CompilerParams(dimension_semantics=("arbitrary",), vmem_limit_bytes=VMEM_LIMIT),
        name="moe_sparse",
    )(tile, ea, eb, lo, hi, first, last, nvis, src, src, x2e, g2, gf, wg, wg, wu, wu, wd, wd)


def _stream(x, c0, n0, m0, cache, pos0, wts):
    (g1, wqkvo, wu, wgt, bgt, poolw, pscale, mhg, wo1, wo2, g2, wr, br, wg, wup, wd, gf) = wts
    n_b, seq_len, _ = x.shape
    xf = x.reshape(n_b * seq_len, D_MODEL)
    cache16 = jnp.pad(cache.astype(F32), ((0, 0), (POOL_ROWS - POOL_STATE, 0), (0, 0)))
    q, k, v, og, gates, hp, pstate = _inproj(xf, cache16, g1, wqkvo, wu, wgt, bgt, poolw, pscale,
                                             seq_len=seq_len, pos0=pos0)
    slot = q.shape[0] // n_b
    cn0 = jnp.concatenate([c0, jnp.broadcast_to(n0[..., None], c0.shape)], axis=-1)
    m0r = jnp.broadcast_to(m0[..., None, None], (n_b, N_HEADS, 8, LANE))
    hm, cn, mo = _mlstm(q, k, v, og, gates, mhg, cn0, m0r, seq_len=slot)
    tm = min(seq_len, 512)
    sparse = n_b * seq_len >= 16 * 512
    if sparse:
        x2e, info = _outproj(hm, hp, xf, wo1, wo2, g2, wr, br, tm=tm, hm_stride=slot // seq_len, sparse=True)
        y = _moe_sparse(x2e, info[:, 2].astype(jnp.int32), g2, gf, wg, wup, wd, tm=512)
    else:
        x2, xn, rg = _outproj(hm, hp, xf, wo1, wo2, g2, wr, br, tm=tm, hm_stride=slot // seq_len, sparse=False)
        y = _moe(xn, x2, rg, wg, wup, wd, gf, tm=min(n_b * seq_len, 512))
    return (y.reshape(n_b, seq_len, D_MODEL), cn[..., :DH], cn[..., DH], mo[:, :, 0, 0],
            pstate[:, POOL_ROWS - POOL_STATE:, :])


def kernel(x_prompt, x_sample, state_mlstm_C, state_mlstm_n, state_mlstm_m, cache_pool, norm1_g, w_in, b_gates,
           mh_norm_g, pool_w, pool_scale, w_out, norm2_g, w_router_group, b_router_group, w_router_expert,
           b_router_expert, w_gate, w_up, w_down, norm_f_g):
    depth = w_in.shape[0]
    assert depth == 1 and x_sample.shape[1] * 4 == LOOKBACK
    n_bp = x_prompt.shape[0]
    xp, xs = x_prompt, x_sample
    outs_p, outs_s = [], []
    for l in range(depth):
        w = w_in[l]
        ng = 2 * N_HEADS
        c_u = 4 * D_MLSTM + ng
        wgt = jnp.pad(w[:, 4 * D_MLSTM:c_u], ((0, 0), (0, LANE - ng))).astype(BF16)
        bgt = jnp.pad(b_gates[l].astype(F32), (0, LANE - ng))[None, :]
        wr = jnp.pad(jnp.concatenate([w_router_expert[l], w_router_group[l]], axis=1),
                     ((0, 0), (0, LANE - N_EXPERTS - N_GROUPS))).astype(BF16)
        br = jnp.pad(jnp.concatenate([b_router_expert[l], b_router_group[l]]).astype(F32),
                     (0, LANE - N_EXPERTS - N_GROUPS))[None, :]
        wts = (norm1_g[l][None, :], w[:, :4 * D_MLSTM].astype(BF16), w[:, c_u:].astype(BF16), wgt, bgt,
               pool_w[l].astype(BF16), pool_scale[l][None, :].astype(F32), mh_norm_g[l][None, :].astype(F32),
               w_out[l][:D_MLSTM].astype(BF16), w_out[l][D_MLSTM:].astype(BF16), norm2_g[l][None, :], wr, br,
               w_gate[l].astype(BF16), w_up[l].astype(BF16), w_down[l].astype(BF16), norm_f_g[None, :])
        c0 = jnp.zeros((n_bp, N_HEADS, DH, DH), F32)
        n0 = jnp.zeros((n_bp, N_HEADS, DH), F32)
        m0 = jnp.full((n_bp, N_HEADS), M_INIT, F32)
        buf0 = jnp.zeros((n_bp, POOL_STATE, D_POOL), F32)
        res_p = _stream(xp, c0, n0, m0, buf0, 0, wts)
        res_s = _stream(xs, state_mlstm_C[l].astype(F32), state_mlstm_n[l].astype(F32),
                        state_mlstm_m[l].astype(F32), cache_pool[l], PAST_LEN, wts)
        xp, xs = res_p[0], res_s[0]
        outs_p.append(res_p[1:])
        outs_s.append(res_s[1:])
    stack = lambda outs, i: jnp.stack([o[i] for o in outs])
    return (xp, xs,
            stack(outs_p, 0), stack(outs_p, 1), stack(outs_p, 2), stack(outs_p, 3),
            stack(outs_s, 0), stack(outs_s, 1), stack(outs_s, 2), stack(outs_s, 3))
```
